```python
import jax
import jax.numpy as jnp
from jax import lax
import numpy as np

D_MODEL = 2048
BATCH = 2
SEQ = 8192
DEPTH = 4

D_MIX = D_MODEL
ML_WIDTH = D_MIX // 4
ML_HEADS = 4
ML_DH = ML_WIDTH // ML_HEADS
MB_WIDTH = D_MIX // 2
MB_HEADDIM = 64
MB_HEADS = MB_WIDTH // MB_HEADDIM
MB_GROUPS = 2
MB_DSTATE = 64
MB_CONV_DIM = MB_WIDTH + 2 * MB_GROUPS * MB_DSTATE
RG_WIDTH = D_MIX - ML_WIDTH - MB_WIDTH
RG_BLOCKS = 8
RG_BDIM = RG_WIDTH // RG_BLOCKS
RG_C = 8.0
CONV_K = 4
CHUNK = 128
FFN_DIM = 5632
N_EXPERTS = 8
TOP_K = 2
MOE_DIM = 7168
N_DENSE = (DEPTH + 1) // 2
N_MOE = DEPTH // 2
EPS = 1e-6
NEG_BIG = -1e30
IN_SIZES = (ML_WIDTH, ML_WIDTH, ML_WIDTH, ML_WIDTH, ML_HEADS, ML_HEADS,
            MB_WIDTH, MB_CONV_DIM, MB_HEADS, RG_WIDTH, RG_WIDTH)
D_IN = 4 * ML_WIDTH + 2 * ML_HEADS + MB_WIDTH + MB_CONV_DIM + MB_HEADS + 2 * RG_WIDTH

kernel_name = 'hybrid_mlstm_ssd_rglru_moe'


def _f32(t):
    return t.astype(jnp.float32)


def _rms(x):
    x = _f32(x)
    return x * lax.rsqrt(jnp.mean(x * x, axis=-1, keepdims=True) + EPS)


def rmsnorm(x, g):
    return (_rms(x) * _f32(g)).astype(x.dtype)


def causal_dwconv(x, w, b):
    k_w = w.shape[0]
    s = x.shape[1]
    xp = jnp.pad(_f32(x), ((0, 0), (k_w - 1, 0), (0, 0)))
    y = _f32(b)
    for j in range(k_w):
        y = y + xp[:, j:j + s] * _f32(w[j])
    return y


def _chunk(t):
    return t.reshape(t.shape[0], t.shape[1] // CHUNK, CHUNK, *t.shape[2:])


def _causal_mask():
    return jnp.tril(jnp.ones((CHUNK, CHUNK), dtype=bool))


def mlstm_chunkwise(q, k, v, i_pre, f_pre):
    bsz, s, nh, dh = q.shape
    q, k, v = _chunk(q), _chunk(k * dh ** -0.5), _chunk(v)
    ig = jnp.swapaxes(_chunk(i_pre), 2, 3)
    bcum = jnp.cumsum(jnp.swapaxes(_chunk(jax.nn.log_sigmoid(f_pre)), 2, 3), axis=-1)
    b_last = bcum[..., -1]
    w_state = b_last[..., None] - bcum + ig
    m_loc = jnp.max(w_state, axis=-1)
    e_state = jnp.exp(w_state - m_loc[..., None])
    kv_chunk = jnp.einsum('bchs,bcshd,bcshe->bchde', e_state, v, k)
    n_chunk = jnp.einsum('bchs,bcshe->bche', e_state, k)

    def step(carry, xs):
        c_st, n_st, m_st = carry
        kv_c, n_c, m_c, bl = xs
        m_new = jnp.maximum(bl + m_st, m_c)
        a = jnp.exp(bl + m_st - m_new)
        g = jnp.exp(m_c - m_new)
        c_new = a[..., None, None] * c_st + g[..., None, None] * kv_c
        n_new = a[..., None] * n_st + g[..., None] * n_c
        return (c_new, n_new, m_new), (c_st, n_st, m_st)

    init = (jnp.zeros((bsz, nh, dh, dh), jnp.float32),
            jnp.zeros((bsz, nh, dh), jnp.float32),
            jnp.full((bsz, nh), NEG_BIG, jnp.float32))
    xs = (jnp.moveaxis(kv_chunk, 1, 0), jnp.moveaxis(n_chunk, 1, 0),
          jnp.moveaxis(m_loc, 1, 0), jnp.moveaxis(b_last, 1, 0))
    _, (c_prev, n_prev, m_prev) = lax.scan(step, init, xs)
    c_prev = jnp.moveaxis(c_prev, 0, 1)
    n_prev = jnp.moveaxis(n_prev, 0, 1)
    m_prev = jnp.moveaxis(m_prev, 0, 1)
    d_log = jnp.where(_causal_mask(), bcum[..., :, None] - bcum[..., None, :] + ig[..., None, :], -jnp.inf)
    g_inter = bcum + m_prev[..., None]
    m_t = jnp.maximum(g_inter, jnp.max(d_log, axis=-1))
    w_intra = jnp.exp(d_log - m_t[..., None]) * jnp.einsum('bcthd,bcshd->bchts', q, k)
    inter = jnp.exp(g_inter - m_t)
    num = (jnp.einsum('bchts,bcshd->bchtd', w_intra, v)
           + inter[..., None] * jnp.einsum('bchde,bcthe->bchtd', c_prev, q))
    den = jnp.sum(w_intra, axis=-1) + inter * jnp.einsum('bche,bcthe->bcht', n_prev, q)
    h = num / jnp.maximum(jnp.abs(den), jnp.exp(-m_t))[..., None]
    return jnp.swapaxes(h, 2, 3).reshape(bsz, s, nh, dh)


def ssd_chunked(x, dt, a_neg, b_in, c_in):
    bsz, s, nh, p = x.shape
    xc = _chunk(x * dt[..., None])
    bc, cc = _chunk(b_in), _chunk(c_in)
    a_cs = jnp.cumsum(jnp.swapaxes(_chunk(dt * a_neg), 2, 3), axis=-1)
    l_mat = jnp.exp(jnp.where(_causal_mask(), a_cs[..., :, None] - a_cs[..., None, :], -jnp.inf))
    scores = jnp.einsum('bclhn,bcshn->bchls', cc, bc) * l_mat
    y_diag = jnp.einsum('bchls,bcshp->bclhp', scores, xc)
    decay_states = jnp.exp(a_cs[..., -1:] - a_cs)
    states = jnp.einsum('bclhn,bchl,bclhp->bchpn', bc, decay_states, xc)
    chunk_decay = jnp.exp(a_cs[..., -1])

    def step(st, xs):
        st_c, dec = xs
        return dec[..., None, None] * st + st_c, st

    _, prev = lax.scan(step, jnp.zeros((bsz, nh, p, b_in.shape[-1]), jnp.float32),
                       (jnp.moveaxis(states, 1, 0), jnp.moveaxis(chunk_decay, 1, 0)))
    prev = jnp.moveaxis(prev, 0, 1)
    y_off = jnp.einsum('bclhn,bchpn,bchl->bclhp', cc, prev, jnp.exp(a_cs))
    return (y_diag + y_off).reshape(bsz, s, nh, p)


def _lin_combine(c1, c2):
    a1, b1 = c1
    a2, b2 = c2
    return a1 * a2, a2 * b1 + b2


def rglru(x, w_r, b_r, w_i, b_i, lam):
    bsz, s, w = x.shape
    xb = x.reshape(bsz, s, RG_BLOCKS, RG_BDIM)
    r = jax.nn.sigmoid(jnp.einsum('bsgi,gij->bsgj', xb, _f32(w_r)).reshape(bsz, s, w) + _f32(b_r))
    i_g = jax.nn.sigmoid(jnp.einsum('bsgi,gij->bsgj', xb, _f32(w_i)).reshape(bsz, s, w) + _f32(b_i))
    log_a = -RG_C * r * jax.nn.softplus(-_f32(lam))
    a = jnp.exp(log_a)
    u = jnp.sqrt(-jnp.expm1(2.0 * log_a)) * (i_g * x)
    _, h = lax.associative_scan(_lin_combine, (a, u), axis=1)
    return h


def hybrid_mixer(xn, w_in, ml_conv_w, ml_conv_b, ml_b_i, ml_b_f, ml_norm_g,
                 mb_conv_w, mb_conv_b, mb_dt_bias, mb_a_log, mb_d, mb_norm_g,
                 rg_conv_w, rg_conv_b, rg_w_r, rg_b_r, rg_w_i, rg_b_i, rg_l, w_out):
    bsz, s, _ = xn.shape
    proj = _f32(xn @ w_in)
    split_at = [int(c) for c in np.cumsum(IN_SIZES)[:-1]]
    (q_pre, k_pre, v, o_pre, i_pre, f_pre, z, xbc, dt_pre, rg_gate, rg_x) = jnp.split(proj, split_at, axis=-1)

    qk = jax.nn.silu(causal_dwconv(jnp.concatenate([q_pre, k_pre], axis=-1), ml_conv_w, ml_conv_b))
    q, k = jnp.split(qk, 2, axis=-1)
    hshape = (bsz, s, ML_HEADS, ML_DH)
    h_ml = mlstm_chunkwise(q.reshape(hshape), k.reshape(hshape), v.reshape(hshape),
                           i_pre + _f32(ml_b_i), f_pre + _f32(ml_b_f))
    y_ml = (_rms(h_ml) * _f32(ml_norm_g).reshape(ML_HEADS, ML_DH)).reshape(bsz, s, ML_WIDTH) * jax.nn.sigmoid(o_pre)

    xbc = jax.nn.silu(causal_dwconv(xbc, mb_conv_w, mb_conv_b))
    x_mb, b_mb, c_mb = jnp.split(xbc, [MB_WIDTH, MB_WIDTH + MB_GROUPS * MB_DSTATE], axis=-1)
    rep = MB_HEADS // MB_GROUPS
    b_mb = jnp.repeat(b_mb.reshape(bsz, s, MB_GROUPS, MB_DSTATE), rep, axis=2)
    c_mb = jnp.repeat(c_mb.reshape(bsz, s, MB_GROUPS, MB_DSTATE), rep, axis=2)
    dt = jax.nn.softplus(dt_pre + _f32(mb_dt_bias))
    x_mb = x_mb.reshape(bsz, s, MB_HEADS, MB_HEADDIM)
    y_mb = ssd_chunked(x_mb, dt, -jnp.exp(_f32(mb_a_log)), b_mb, c_mb) + x_mb * _f32(mb_d)[:, None]
    y_mb = y_mb.reshape(bsz, s, MB_WIDTH) * jax.nn.silu(z)
    y_mb = _rms(y_mb.reshape(bsz, s, MB_GROUPS, MB_WIDTH // MB_GROUPS)).reshape(bsz, s, MB_WIDTH) * _f32(mb_norm_g)

    h_rg = rglru(causal_dwconv(rg_x, rg_conv_w, rg_conv_b), rg_w_r, rg_b_r, rg_w_i, rg_b_i, rg_l)
    y_rg = jax.nn.gelu(rg_gate) * h_rg

    y = jnp.concatenate([y_ml, y_mb, y_rg], axis=-1).astype(xn.dtype)
    return y @ w_out


def swiglu(t, w1, w3, w2):
    return (jax.nn.silu(t @ w1) * (t @ w3)) @ w2


def moe_swiglu(xn, router, w1, w3, w2):
    bsz, s, d = xn.shape
    t = xn.reshape(bsz * s, d)
    logits = _f32(t @ router)
    top_v, top_i = lax.top_k(logits, TOP_K)
    gates = jax.nn.softmax(top_v, axis=-1)
    combine = jnp.sum(jax.nn.one_hot(top_i, N_EXPERTS, dtype=jnp.float32) * gates[..., None], axis=1)
    out = jnp.zeros_like(t)
    for e in range(N_EXPERTS):
        out = out + combine[:, e:e + 1].astype(t.dtype) * swiglu(t, w1[e], w3[e], w2[e])
    return out.reshape(bsz, s, d)


def setup_inputs(seed: int = 0) -> dict:
    key = jax.random.key(seed)
    ks = jax.random.split(key, 31)

    def nrm(i, shape, scale):
        return jax.random.normal(ks[i], shape, jnp.float32) * scale

    def unif(i, shape, lo, hi):
        return jax.random.uniform(ks[i], shape, jnp.float32, lo, hi)

    dt0 = jnp.exp(unif(10, (DEPTH, MB_HEADS), float(np.log(1e-3)), float(np.log(1e-1))))
    a0 = unif(20, (DEPTH, RG_WIDTH), 0.9, 0.999) ** (1.0 / RG_C)
    return {
        'x': nrm(0, (BATCH, SEQ, D_MODEL), 1.0),
        'norm_mix_g': 1.0 + nrm(1, (DEPTH, D_MODEL), 0.02),
        'w_in': nrm(2, (DEPTH, D_MODEL, D_IN), D_MODEL ** -0.5),
        'ml_conv_w': nrm(3, (DEPTH, CONV_K, 2 * ML_WIDTH), CONV_K ** -0.5),
        'ml_conv_b': nrm(4, (DEPTH, 2 * ML_WIDTH), 0.02),
        'ml_b_i': nrm(5, (DEPTH, ML_HEADS), 0.1),
        'ml_b_f': jnp.linspace(3.0, 6.0, ML_HEADS, dtype=jnp.float32) + nrm(6, (DEPTH, ML_HEADS), 0.1),
        'ml_norm_g': 1.0 + nrm(7, (DEPTH, ML_WIDTH), 0.02),
        'mb_conv_w': nrm(8, (DEPTH, CONV_K, MB_CONV_DIM), CONV_K ** -0.5),
        'mb_conv_b': nrm(9, (DEPTH, MB_CONV_DIM), 0.02),
        'mb_dt_bias': dt0 + jnp.log(-jnp.expm1(-dt0)),
        'mb_a_log': jnp.log(unif(11, (DEPTH, MB_HEADS), 1.0, 16.0)),
        'mb_d': 1.0 + nrm(12, (DEPTH, MB_HEADS), 0.02),
        'mb_norm_g': 1.0 + nrm(13, (DEPTH, MB_WIDTH), 0.02),
        'rg_conv_w': nrm(14, (DEPTH, CONV_K, RG_WIDTH), CONV_K ** -0.5),
        'rg_conv_b': nrm(15, (DEPTH, RG_WIDTH), 0.02),
        'rg_w_r': nrm(16, (DEPTH, RG_BLOCKS, RG_BDIM, RG_BDIM), RG_BDIM ** -0.5),
        'rg_b_r': nrm(17, (DEPTH, RG_WIDTH), 0.02),
        'rg_w_i': nrm(18, (DEPTH, RG_BLOCKS, RG_BDIM, RG_BDIM), RG_BDIM ** -0.5),
        'rg_b_i': nrm(19, (DEPTH, RG_WIDTH), 0.02),
        'rg_l': jnp.log(a0) - jnp.log1p(-a0),
        'w_out': nrm(21, (DEPTH, D_MIX, D_MODEL), D_MIX ** -0.5),
        'norm_ffn_g': 1.0 + nrm(22, (DEPTH, D_MODEL), 0.02),
        'ffn_w1': nrm(23, (N_DENSE, D_MODEL, FFN_DIM), D_MODEL ** -0.5),
        'ffn_w3': nrm(24, (N_DENSE, D_MODEL, FFN_DIM), D_MODEL ** -0.5),
        'ffn_w2': nrm(25, (N_DENSE, FFN_DIM, D_MODEL), FFN_DIM ** -0.5),
        'moe_router': nrm(26, (N_MOE, D_MODEL, N_EXPERTS), D_MODEL ** -0.5),
        'moe_w1': nrm(27, (N_MOE, N_EXPERTS, D_MODEL, MOE_DIM), D_MODEL ** -0.5),
        'moe_w3': nrm(28, (N_MOE, N_EXPERTS, D_MODEL, MOE_DIM), D_MODEL ** -0.5),
        'moe_w2': nrm(29, (N_MOE, N_EXPERTS, MOE_DIM, D_MODEL), MOE_DIM ** -0.5),
        'norm_final_g': 1.0 + nrm(30, (D_MODEL,), 0.02),
    }


def reference(x, norm_mix_g, w_in, ml_conv_w, ml_conv_b, ml_b_i, ml_b_f, ml_norm_g,
              mb_conv_w, mb_conv_b, mb_dt_bias, mb_a_log, mb_d, mb_norm_g,
              rg_conv_w, rg_conv_b, rg_w_r, rg_b_r, rg_w_i, rg_b_i, rg_l, w_out,
              norm_ffn_g, ffn_w1, ffn_w3, ffn_w2, moe_router, moe_w1, moe_w3, moe_w2,
              norm_final_g):
    for l in range(DEPTH):
        h = rmsnorm(x, norm_mix_g[l])
        x = x + hybrid_mixer(h, w_in[l], ml_conv_w[l], ml_conv_b[l], ml_b_i[l], ml_b_f[l], ml_norm_g[l],
                             mb_conv_w[l], mb_conv_b[l], mb_dt_bias[l], mb_a_log[l], mb_d[l], mb_norm_g[l],
                             rg_conv_w[l], rg_conv_b[l], rg_w_r[l], rg_b_r[l], rg_w_i[l], rg_b_i[l], rg_l[l],
                             w_out[l]).astype(x.dtype)
        h = rmsnorm(x, norm_ffn_g[l])
        if l % 2 == 0:
            x = x + swiglu(h, ffn_w1[l // 2], ffn_w3[l // 2], ffn_w2[l // 2])
        else:
            x = x + moe_swiglu(h, moe_router[l // 2], moe_w1[l // 2], moe_w3[l // 2], moe_w2[l // 2])
    return rmsnorm(x, norm_final_g)
```

```python
import functools

import numpy as np
import jax
import jax.numpy as jnp
from jax import lax
from jax.experimental import pallas as pl
from jax.experimental.pallas import tpu as pltpu

F32 = jnp.float32
BF16 = jnp.bfloat16
HIGHEST = lax.Precision.HIGHEST

D_MODEL = 2048
ML_WIDTH = 512
ML_HEADS = 4
ML_DH = 128
MB_WIDTH = 1024
MB_HEADS = 16
MB_HEADDIM = 64
MB_GROUPS = 2
MB_DSTATE = 64
MB_CONV_DIM = MB_WIDTH + 2 * MB_GROUPS * MB_DSTATE
RG_WIDTH = 512
RG_BLOCKS = 8
RG_BDIM = 64
RG_C = 8.0
CONV_K = 4
N_EXPERTS = 8
EPS = 1e-6
NEG_BIG = -1e30

LANES = 128
SUBLANES = 8
CHUNK = 128
VMEM_LIMIT = 52 * 1024 * 1024

OFF_Q = 0
OFF_K = 512
OFF_V = 1024
OFF_O = 1536
OFF_Z = 2048
OFF_XBC = 3072
OFF_RGG = 4352
OFF_RGX = 4864
OFF_GATE = 5376
N_USED = 5504
N_PROJ = 5632
G_I = 0
G_F = 4
G_DTA = 8
G_DT = 24
CV_QK = 0
CV_XBC = 1024
CV_RG = 2304
CV_TOTAL = 2816


def _rms_scale(x):
    return lax.rsqrt(jnp.mean(x * x, axis=-1, keepdims=True) + EPS)


def _softplus(x):
    return jnp.maximum(x, 0.0) + jnp.log1p(jnp.exp(-jnp.abs(x)))


def _sigmoid(x):
    return 1.0 / (1.0 + jnp.exp(-x))


def _silu(x):
    return x * _sigmoid(x)


def _gelu_tanh(x):
    c = float(np.sqrt(2.0 / np.pi))
    return 0.5 * x * (1.0 + jnp.tanh(c * (x + 0.044715 * (x * x * x))))


def _dot(a, b):
    return jnp.dot(a.astype(BF16), b.astype(BF16), preferred_element_type=F32)


def _dot_f32(a, b):
    return jnp.dot(a, b, preferred_element_type=F32, precision=HIGHEST)


def _norm_matmul_kernel(x_ref, g_ref, w_ref, o_ref, h_ref):
    @pl.when(pl.program_id(1) == 0)
    def _():
        x = x_ref[...]
        h_ref[...] = (x * _rms_scale(x) * g_ref[0]).astype(BF16)

    o_ref[...] = jnp.dot(h_ref[...], w_ref[0], preferred_element_type=F32)


def _norm_matmul(x, g_all, w_all, layer, tm, tn):
    t, d = x.shape
    n = w_all.shape[-1]
    return pl.pallas_call(
        _norm_matmul_kernel,
        grid=(t // tm, n // tn),
        in_specs=[
            pl.BlockSpec((tm, d), lambda i, j: (i, 0)),
            pl.BlockSpec((1, 1, d), lambda i, j: (layer, 0, 0)),
            pl.BlockSpec((1, d, tn), lambda i, j: (layer, 0, j)),
        ],
        out_specs=pl.BlockSpec((tm, tn), lambda i, j: (i, j)),
        out_shape=jax.ShapeDtypeStruct((t, n), F32),
        scratch_shapes=[pltpu.VMEM((tm, d), BF16)],
        compiler_params=pltpu.CompilerParams(
            dimension_semantics=("arbitrary", "arbitrary"), vmem_limit_bytes=VMEM_LIMIT),
        name="in_proj",
    )(x, g_all, w_all)


def _matmul_residual_kernel(y_ref, w_ref, x_ref, o_ref):
    o_ref[...] = x_ref[...] + jnp.dot(y_ref[...], w_ref[0], preferred_element_type=F32)


def _matmul_residual(y, w_all, x, layer, tm, tn):
    t, k = y.shape
    n = w_all.shape[-1]
    return pl.pallas_call(
        _matmul_residual_kernel,
        grid=(t // tm, n // tn),
        in_specs=[
            pl.BlockSpec((tm, k), lambda i, j: (i, 0)),
            pl.BlockSpec((1, k, tn), lambda i, j: (layer, 0, j)),
            pl.BlockSpec((tm, tn), lambda i, j: (i, j)),
        ],
        out_specs=pl.BlockSpec((tm, tn), lambda i, j: (i, j)),
        out_shape=jax.ShapeDtypeStruct((t, n), F32),
        compiler_params=pltpu.CompilerParams(
            dimension_semantics=("arbitrary", "arbitrary"), vmem_limit_bytes=VMEM_LIMIT),
        name="out_proj",
    )(y, w_all, x)


def _mixer_kernel(proj_ref, cw_ref, cb_ref, gbias_ref, alog_ref, mlg_ref, mbd_ref, mbg_ref,
                  wri_ref, bri_ref, rgl_ref, ehead_ref,
                  y_ref,
                  cbuf_ref, ct_ref, m_ref, st_ref, hrg_ref, ymb_ref):
    L = CHUNK
    c = pl.program_id(1)

    @pl.when(c == 0)
    def _():
        cbuf_ref[0:SUBLANES, :] = jnp.zeros((SUBLANES, CV_TOTAL), F32)
        ct_ref[...] = jnp.zeros(ct_ref.shape, F32)
        m_ref[...] = jnp.full(m_ref.shape, NEG_BIG, F32)
        st_ref[...] = jnp.zeros(st_ref.shape, F32)
        hrg_ref[...] = jnp.zeros(hrg_ref.shape, F32)

    cbuf_ref[SUBLANES:SUBLANES + L, CV_QK:CV_XBC] = proj_ref[:, OFF_Q:OFF_Q + 1024]
    cbuf_ref[SUBLANES:SUBLANES + L, CV_XBC:CV_RG] = proj_ref[:, OFF_XBC:OFF_XBC + MB_CONV_DIM]
    cbuf_ref[SUBLANES:SUBLANES + L, CV_RG:CV_TOTAL] = proj_ref[:, OFF_RGX:OFF_RGX + RG_WIDTH]

    def conv(lo, hi):
        acc = cb_ref[0, :, lo:hi]
        for j in range(CONV_K):
            shift = CONV_K - 1 - j
            acc = acc + cbuf_ref[SUBLANES - shift:SUBLANES - shift + L, lo:hi] * cw_ref[0, j:j + 1, lo:hi]
        return acc

    qk = _silu(conv(CV_QK, CV_XBC))
    xbc = _silu(conv(CV_XBC, CV_RG))
    xrg = conv(CV_RG, CV_TOTAL)
    cbuf_ref[0:SUBLANES, :] = cbuf_ref[L:L + SUBLANES, :]

    lane = lax.broadcasted_iota(jnp.int32, (L, LANES), 1)
    gpre = proj_ref[:, OFF_GATE:OFF_GATE + LANES] + gbias_ref[0]
    a_neg = -jnp.exp(alog_ref[0])
    is_f = (lane >= G_F) & (lane < G_DTA)
    is_dta = (lane >= G_DTA) & (lane < G_DT)
    is_dt = (lane >= G_DT) & (lane < G_DT + MB_HEADS)
    sp = _softplus(gpre)
    xg = jnp.where(is_f, -_softplus(-gpre),
                   jnp.where(is_dta, sp * a_neg, jnp.where(is_dt, sp, gpre)))
    row_i = lax.broadcasted_iota(jnp.int32, (L, L), 0)
    col_i = lax.broadcasted_iota(jnp.int32, (L, L), 1)
    causal = row_i >= col_i
    tril = jnp.where(causal, 1.0, 0.0).astype(F32)
    cum = _dot_f32(tril, xg)
    xg_t = xg.T
    cum_t = cum.T

    ones_col = jnp.where(lane == 0, 1.0, 0.0).astype(F32)
    for h in range(ML_HEADS):
        q_h = qk[:, h * ML_DH:(h + 1) * ML_DH]
        k_h = qk[:, ML_WIDTH + h * ML_DH:ML_WIDTH + (h + 1) * ML_DH] * (ML_DH ** -0.5)
        v_h = proj_ref[:, OFF_V + h * ML_DH:OFF_V + (h + 1) * ML_DH]
        o_h = proj_ref[:, OFF_O + h * ML_DH:OFF_O + (h + 1) * ML_DH]
        vaug = jnp.concatenate([v_h, ones_col], axis=-1).astype(BF16)
        kt = k_h.T
        bc_col = cum[:, G_F + h:G_F + h + 1]
        bc_row = cum_t[G_F + h:G_F + h + 1, :]
        ig_row = xg_t[G_I + h:G_I + h + 1, :]
        m_prev = m_ref[h:h + 1, 0:1]
        dlog = jnp.where(causal, bc_col - bc_row + ig_row, -jnp.inf)
        rmax = jnp.max(dlog, axis=-1, keepdims=True)
        g_inter = bc_col + m_prev
        m_t = jnp.maximum(g_inter, rmax)
        w_intra = jnp.exp(dlog - m_t) * _dot(q_h, kt)
        inter = jnp.exp(g_inter - m_t)
        ct_prev = ct_ref[h]
        nd = _dot(w_intra, vaug) + inter * _dot(q_h, ct_prev)
        num = nd[:, 0:ML_DH]
        den = nd[:, ML_DH:ML_DH + 1]
        hh = num / jnp.maximum(jnp.abs(den), jnp.exp(-m_t))
        y_h = hh * _rms_scale(hh) * mlg_ref[0, 0:1, h * ML_DH:(h + 1) * ML_DH] * _sigmoid(o_h)
        y_ref[:, h * ML_DH:(h + 1) * ML_DH] = y_h.astype(y_ref.dtype)
        b_last = bc_row[:, L - 1:L]
        w_state = b_last - bc_row + ig_row
        m_loc = jnp.max(w_state, axis=-1, keepdims=True)
        e_row = jnp.exp(w_state - m_loc)
        kvt = _dot(kt * e_row, vaug)
        m_new = jnp.maximum(b_last + m_prev, m_loc)
        a_sc = jnp.exp(b_last + m_prev - m_new)
        g_sc = jnp.exp(m_loc - m_new)
        ct_ref[h] = a_sc * ct_prev + g_sc * kvt
        m_ref[h:h + 1, :] = jnp.broadcast_to(m_new, (1, LANES))

    x_mb = xbc[:, 0:MB_WIDTH]
    b_mb = xbc[:, MB_WIDTH:MB_WIDTH + LANES]
    c_mb = xbc[:, MB_WIDTH + LANES:MB_WIDTH + 2 * LANES]
    bt = b_mb.T
    a_last = cum[L - 1:L, :]
    dec_g = jnp.exp(jnp.where(is_dta, a_last - cum, 0.0))
    expa_g = jnp.exp(jnp.where(is_dta, cum, 0.0))
    e_dta = ehead_ref[0]
    e_dt = ehead_ref[1]
    dt_x = _dot_f32(jnp.where(is_dt, xg, 0.0), e_dt)
    dec_x = _dot_f32(dec_g, e_dta)
    expa_x = _dot_f32(expa_g, e_dta)
    xdt = x_mb * dt_x
    xdt_dec = xdt * dec_x
    half = lax.broadcasted_iota(jnp.int32, (L, LANES), 1) < MB_HEADDIM
    per_group = MB_HEADS // MB_GROUPS
    gw = per_group * MB_HEADDIM
    for g in range(MB_GROUPS):
        c_g = c_mb[:, g * MB_DSTATE:(g + 1) * MB_DSTATE]
        bt_g = bt[g * MB_DSTATE:(g + 1) * MB_DSTATE, :]
        cb = _dot(c_g, bt_g)
        st_prev = st_ref[g]
        y_off = _dot(c_g, st_prev) * expa_x[:, g * gw:(g + 1) * gw]
        cdec = expa_x[L - 1:L, g * gw:(g + 1) * gw]
        st_ref[g] = cdec * st_prev + _dot(bt_g, xdt_dec[:, g * gw:(g + 1) * gw])
        for p in range(per_group // 2):
            lo = g * gw + p * LANES
            xp = xdt[:, lo:lo + LANES]
            acc = None
            for s in range(2):
                hd = g * per_group + 2 * p + s
                a_col = cum[:, G_DTA + hd:G_DTA + hd + 1]
                a_row = cum_t[G_DTA + hd:G_DTA + hd + 1, :]
                lmat = jnp.exp(jnp.where(causal, a_col - a_row, -jnp.inf))
                xs = jnp.where(half if s == 0 else jnp.logical_not(half), xp, 0.0)
                part = _dot(cb * lmat, xs)
                acc = part if acc is None else acc + part
            y_pair = acc + y_off[:, p * LANES:(p + 1) * LANES] + x_mb[:, lo:lo + LANES] * mbd_ref[0, 0:1, lo:lo + LANES]
            zz = proj_ref[:, OFF_Z + lo:OFF_Z + lo + LANES]
            ymb_ref[:, lo:lo + LANES] = y_pair * _silu(zz)
    for g in range(MB_GROUPS):
        yg = ymb_ref[:, g * gw:(g + 1) * gw]
        yn = yg * _rms_scale(yg) * mbg_ref[0, 0:1, g * gw:(g + 1) * gw]
        y_ref[:, ML_WIDTH + g * gw:ML_WIDTH + (g + 1) * gw] = yn.astype(y_ref.dtype)

    ri = _dot(xrg, wri_ref[0]) + bri_ref[0]
    r_g = _sigmoid(ri[:, 0:RG_WIDTH])
    i_g = _sigmoid(ri[:, RG_WIDTH:2 * RG_WIDTH])
    log_a = -RG_C * r_g * _softplus(-rgl_ref[0])
    a = jnp.exp(log_a)
    u = jnp.sqrt(1.0 - jnp.exp(2.0 * log_a)) * (i_g * xrg)
    rows = lax.broadcasted_iota(jnp.int32, (L, RG_WIDTH), 0)
    s = 1
    while s < L:
        keep = rows >= s
        a_sh = jnp.where(keep, pltpu.roll(a, s, 0), 1.0)
        u_sh = jnp.where(keep, pltpu.roll(u, s, 0), 0.0)
        u = u + a * u_sh
        a = a * a_sh
        s *= 2
    h_rg = a * hrg_ref[0:1, :] + u
    hrg_ref[0:1, :] = h_rg[L - 1:L, :]
    g_rg = proj_ref[:, OFF_RGG:OFF_RGG + RG_WIDTH]
    y_ref[:, ML_WIDTH + MB_WIDTH:D_MODEL] = (_gelu_tanh(g_rg) * h_rg).astype(y_ref.dtype)


def _mixer(proj, params, layer, batch, seq):
    nc = seq // CHUNK
    vec = lambda n: pl.BlockSpec((1, 1, n), lambda b, c: (layer, 0, 0))
    in_specs = [
        pl.BlockSpec((CHUNK, N_USED), lambda b, c: (b * nc + c, 0)),
        pl.BlockSpec((1, CONV_K, CV_TOTAL), lambda b, c: (layer, 0, 0)),
        vec(CV_TOTAL),
        vec(LANES),
        vec(LANES),
        vec(ML_WIDTH),
        vec(MB_WIDTH),
        vec(MB_WIDTH),
        pl.BlockSpec((1, RG_WIDTH, 2 * RG_WIDTH), lambda b, c: (layer, 0, 0)),
        vec(2 * RG_WIDTH),
        vec(RG_WIDTH),
        pl.BlockSpec((2, LANES, MB_WIDTH), lambda b, c: (0, 0, 0)),
    ]
    return pl.pallas_call(
        _mixer_kernel,
        grid=(batch, nc),
        in_specs=in_specs,
        out_specs=pl.BlockSpec((CHUNK, D_MODEL), lambda b, c: (b * nc + c, 0)),
        out_shape=jax.ShapeDtypeStruct((batch * seq, D_MODEL), BF16),
        scratch_shapes=[
            pltpu.VMEM((CHUNK + 2 * SUBLANES, CV_TOTAL), F32),
            pltpu.VMEM((ML_HEADS, ML_DH, 2 * ML_DH), F32),
            pltpu.VMEM((SUBLANES, LANES), F32),
            pltpu.VMEM((MB_GROUPS, MB_DSTATE, MB_WIDTH // MB_GROUPS), F32),
            pltpu.VMEM((SUBLANES, RG_WIDTH), F32),
            pltpu.VMEM((CHUNK, MB_WIDTH), F32),
        ],
        compiler_params=pltpu.CompilerParams(
            dimension_semantics=("arbitrary", "arbitrary"), vmem_limit_bytes=VMEM_LIMIT),
        name="mixer",
    )(proj, params["conv_w"], params["conv_b"], params["gate_bias"], params["a_log"],
      params["ml_norm_g"], params["mb_d"], params["mb_norm_g"], params["rg_wri"],
      params["rg_bri"], params["rg_l"], params["ehead"])


def _ffn_kernel(x_ref, g_ref, w1_ref, w3_ref, w2_ref, o_ref, h_ref):
    @pl.when(pl.program_id(1) == 0)
    def _():
        x = x_ref[...]
        h_ref[...] = (x * _rms_scale(x) * g_ref[0]).astype(BF16)
        o_ref[...] = x

    h = h_ref[...]
    a = jnp.dot(h, w1_ref[0], preferred_element_type=F32)
    b = jnp.dot(h, w3_ref[0], preferred_element_type=F32)
    act = (_silu(a) * b).astype(BF16)
    o_ref[...] += jnp.dot(act, w2_ref[0], preferred_element_type=F32)


def _ffn(x, g_all, w1_all, w3_all, w2_all, layer, idx, tm, tf):
    t, d = x.shape
    f = w1_all.shape[-1]
    return pl.pallas_call(
        _ffn_kernel,
        grid=(t // tm, f // tf),
        in_specs=[
            pl.BlockSpec((tm, d), lambda i, j: (i, 0)),
            pl.BlockSpec((1, 1, d), lambda i, j: (layer, 0, 0)),
            pl.BlockSpec((1, d, tf), lambda i, j: (idx, 0, j)),
            pl.BlockSpec((1, d, tf), lambda i, j: (idx, 0, j)),
            pl.BlockSpec((1, tf, d), lambda i, j: (idx, j, 0)),
        ],
        out_specs=pl.BlockSpec((tm, d), lambda i, j: (i, 0)),
        out_shape=jax.ShapeDtypeStruct((t, d), F32),
        scratch_shapes=[pltpu.VMEM((tm, d), BF16)],
        compiler_params=pltpu.CompilerParams(
            dimension_semantics=("arbitrary", "arbitrary"), vmem_limit_bytes=VMEM_LIMIT),
        name="ffn",
    )(x, g_all, w1_all, w3_all, w2_all)


M_E1, M_E2, M_R1, M_R2, M_G1, M_G2 = 0, 1, 2, 3, 4, 5


def _router_kernel(x_ref, g_ref, wr_ref, meta_ref, cnt_ref, run_ref):
    tr = x_ref.shape[0]

    @pl.when(pl.program_id(0) == 0)
    def _():
        run_ref[...] = jnp.zeros(run_ref.shape, F32)

    x = x_ref[...]
    h = x * _rms_scale(x) * g_ref[0]
    logits = _dot_f32(h, wr_ref[0])
    lane = lax.broadcasted_iota(jnp.int32, (tr, LANES), 1)
    logits = jnp.where(lane < N_EXPERTS, logits, -jnp.inf)
    m1 = jnp.max(logits, axis=-1, keepdims=True)
    i1 = jnp.min(jnp.where(logits == m1, lane, LANES), axis=-1, keepdims=True)
    rest = jnp.where(lane == i1, -jnp.inf, logits)
    m2 = jnp.max(rest, axis=-1, keepdims=True)
    i2 = jnp.min(jnp.where(rest == m2, lane, LANES), axis=-1, keepdims=True)
    e2 = jnp.exp(m2 - m1)
    g1 = 1.0 / (1.0 + e2)
    g2 = e2 / (1.0 + e2)
    oh1 = lane == i1
    oh2 = lane == i2
    oh = jnp.where(oh1 | oh2, 1.0, 0.0).astype(F32)
    rr = lax.broadcasted_iota(jnp.int32, (tr, tr), 0)
    cc = lax.broadcasted_iota(jnp.int32, (tr, tr), 1)
    strict = jnp.where(rr > cc, 1.0, 0.0).astype(BF16)
    prefix = jnp.dot(strict, oh.astype(BF16), preferred_element_type=F32) + run_ref[0:1, :]
    r1 = jnp.sum(jnp.where(oh1, prefix, 0.0), axis=-1, keepdims=True)
    r2 = jnp.sum(jnp.where(oh2, prefix, 0.0), axis=-1, keepdims=True)
    run_ref[0:1, :] = run_ref[0:1, :] + jnp.sum(oh, axis=0, keepdims=True)
    meta = jnp.where(lane == M_E1, i1.astype(F32),
           jnp.where(lane == M_E2, i2.astype(F32),
           jnp.where(lane == M_R1, r1,
           jnp.where(lane == M_R2, r2,
           jnp.where(lane == M_G1, g1,
           jnp.where(lane == M_G2, g2, 0.0))))))
    meta_ref[...] = meta
    cnt_ref[...] = run_ref[...]


def _router(x, g_all, wr_all, layer, idx, tr):
    t, d = x.shape
    return pl.pallas_call(
        _router_kernel,
        grid=(t // tr,),
        in_specs=[
            pl.BlockSpec((tr, d), lambda i: (i, 0)),
            pl.BlockSpec((1, 1, d), lambda i: (layer, 0, 0)),
            pl.BlockSpec((1, d, LANES), lambda i: (idx, 0, 0)),
        ],
        out_specs=[
            pl.BlockSpec((tr, LANES), lambda i: (i, 0)),
            pl.BlockSpec((SUBLANES, LANES), lambda i: (0, 0)),
        ],
        out_shape=[jax.ShapeDtypeStruct((t, LANES), F32),
                   jax.ShapeDtypeStruct((SUBLANES, LANES), F32)],
        scratch_shapes=[pltpu.VMEM((SUBLANES, LANES), F32)],
        compiler_params=pltpu.CompilerParams(
            dimension_semantics=("arbitrary",), vmem_limit_bytes=VMEM_LIMIT),
        name="router",
    )(x, g_all, wr_all)


def _row_copy(src_hbm, src_row, dst_hbm, dst_row, sem):
    return pltpu.make_async_copy(src_hbm.at[pl.ds(src_row, 1)], dst_hbm.at[pl.ds(dst_row, 1)], sem)


def _scatter_kernel(p1_ref, p2_ref, x_hbm, zeros_hbm, xs_hbm, sem):
    del zeros_hbm
    ts = p1_ref.shape[-1]
    base = pl.program_id(0) * ts

    def start(r, carry):
        _row_copy(x_hbm, base + r, xs_hbm, p1_ref[0, 0, r], sem).start()
        _row_copy(x_hbm, base + r, xs_hbm, p2_ref[0, 0, r], sem).start()
        return carry

    lax.fori_loop(0, ts, start, 0)

    def wait(r, carry):
        _row_copy(x_hbm, 0, xs_hbm, 0, sem).wait()
        _row_copy(x_hbm, 0, xs_hbm, 0, sem).wait()
        return carry

    lax.fori_loop(0, ts, wait, 0)


def _scatter_rows(x, pos1, pos2, n_rows, ts):
    t, d = x.shape
    nt = t // ts
    zeros = jnp.zeros((n_rows, d), x.dtype)
    smem_blk = pl.BlockSpec((1, 1, ts), lambda i: (i, 0, 0), memory_space=pltpu.SMEM)
    return pl.pallas_call(
        _scatter_kernel,
        grid=(nt,),
        in_specs=[smem_blk, smem_blk,
                  pl.BlockSpec(memory_space=pl.ANY),
                  pl.BlockSpec(memory_space=pl.ANY)],
        out_specs=pl.BlockSpec(memory_space=pl.ANY),
        out_shape=jax.ShapeDtypeStruct((n_rows, d), x.dtype),
        scratch_shapes=[pltpu.SemaphoreType.DMA(())],
        input_output_aliases={3: 0},
        compiler_params=pltpu.CompilerParams(
            dimension_semantics=("arbitrary",), has_side_effects=True),
        name="scatter_rows",
    )(pos1.reshape(nt, 1, ts), pos2.reshape(nt, 1, ts), x, zeros)


def _expert_kernel(te_ref, tv_ref, xs_ref, g_ref, w1_ref, w3_ref, w2_ref, o_ref, h_ref):
    i = pl.program_id(0)
    j = pl.program_id(1)
    valid = tv_ref[i] > 0

    @pl.when(j == 0)
    def _():
        o_ref[...] = jnp.zeros(o_ref.shape, F32)

    @pl.when(jnp.logical_and(j == 0, valid))
    def _():
        x = xs_ref[...]
        h_ref[...] = (x * _rms_scale(x) * g_ref[0]).astype(BF16)

    @pl.when(valid)
    def _():
        h = h_ref[...]
        a = jnp.dot(h, w1_ref[0, 0], preferred_element_type=F32)
        b = jnp.dot(h, w3_ref[0, 0], preferred_element_type=F32)
        act = (_silu(a) * b).astype(BF16)
        o_ref[...] += jnp.dot(act, w2_ref[0, 0], preferred_element_type=F32)


def _experts(xs, tile_expert, tile_valid, g_all, w1_all, w3_all, w2_all, layer, idx, tm, tf):
    p, d = xs.shape
    f = w1_all.shape[-1]
    nf = f // tf

    def jj(i, j, tv):
        return jnp.where(tv[i] > 0, j, nf - 1)

    grid_spec = pltpu.PrefetchScalarGridSpec(
        num_scalar_prefetch=2,
        grid=(p // tm, nf),
        in_specs=[
            pl.BlockSpec((tm, d), lambda i, j, te, tv: (i, 0)),
            pl.BlockSpec((1, 1, d), lambda i, j, te, tv: (layer, 0, 0)),
            pl.BlockSpec((1, 1, d, tf), lambda i, j, te, tv: (idx, te[i], 0, jj(i, j, tv))),
            pl.BlockSpec((1, 1, d, tf), lambda i, j, te, tv: (idx, te[i], 0, jj(i, j, tv))),
            pl.BlockSpec((1, 1, tf, d), lambda i, j, te, tv: (idx, te[i], jj(i, j, tv), 0)),
        ],
        out_specs=pl.BlockSpec((tm, d), lambda i, j, te, tv: (i, 0)),
        scratch_shapes=[pltpu.VMEM((tm, d), BF16)],
    )
    return pl.pallas_call(
        _expert_kernel,
        grid_spec=grid_spec,
        out_shape=jax.ShapeDtypeStruct((p, d), F32),
        compiler_params=pltpu.CompilerParams(
            dimension_semantics=("arbitrary", "arbitrary"), vmem_limit_bytes=VMEM_LIMIT),
        name="experts",
    )(tile_expert, tile_valid, xs, g_all, w1_all, w3_all, w2_all)


def _combine_kernel(p1_ref, p2_ref, x_ref, meta_ref, ys_hbm, o_ref, buf_ref, sem):
    tc = x_ref.shape[0]

    def start(r, carry):
        pltpu.make_async_copy(ys_hbm.at[pl.ds(p1_ref[0, 0, r], 1)], buf_ref.at[0, pl.ds(r, 1)], sem).start()
        pltpu.make_async_copy(ys_hbm.at[pl.ds(p2_ref[0, 0, r], 1)], buf_ref.at[1, pl.ds(r, 1)], sem).start()
        return carry

    lax.fori_loop(0, tc, start, 0)

    def wait(r, carry):
        pltpu.make_async_copy(ys_hbm.at[pl.ds(0, 1)], buf_ref.at[0, pl.ds(0, 1)], sem).wait()
        pltpu.make_async_copy(ys_hbm.at[pl.ds(0, 1)], buf_ref.at[1, pl.ds(0, 1)], sem).wait()
        return carry

    lax.fori_loop(0, tc, wait, 0)
    g1 = meta_ref[:, M_G1:M_G1 + 1]
    g2 = meta_ref[:, M_G2:M_G2 + 1]
    o_ref[...] = x_ref[...] + g1 * buf_ref[0] + g2 * buf_ref[1]


def _combine(x, meta, ys, pos1, pos2, tc):
    t, d = x.shape
    nt = t // tc
    smem_blk = pl.BlockSpec((1, 1, tc), lambda i: (i, 0, 0), memory_space=pltpu.SMEM)
    return pl.pallas_call(
        _combine_kernel,
        grid=(nt,),
        in_specs=[smem_blk, smem_blk,
                  pl.BlockSpec((tc, d), lambda i: (i, 0)),
                  pl.BlockSpec((tc, LANES), lambda i: (i, 0)),
                  pl.BlockSpec(memory_space=pl.ANY)],
        out_specs=pl.BlockSpec((tc, d), lambda i: (i, 0)),
        out_shape=jax.ShapeDtypeStruct((t, d), F32),
        scratch_shapes=[pltpu.VMEM((2, tc, d), F32), pltpu.SemaphoreType.DMA(())],
        compiler_params=pltpu.CompilerParams(
            dimension_semantics=("arbitrary",), vmem_limit_bytes=VMEM_LIMIT),
        name="combine",
    )(pos1.reshape(nt, 1, tc), pos2.reshape(nt, 1, tc), x, meta, ys)


def _routing_tables(meta, counts, tm, n_tiles):
    e1 = meta[:, M_E1].astype(jnp.int32)
    e2 = meta[:, M_E2].astype(jnp.int32)
    r1 = meta[:, M_R1].astype(jnp.int32)
    r2 = meta[:, M_R2].astype(jnp.int32)
    cnt = counts[0, :N_EXPERTS].astype(jnp.int32)
    padded = ((cnt + tm - 1) // tm) * tm
    ends = jnp.cumsum(padded)
    offsets = ends - padded
    pos1 = offsets[e1] + r1
    pos2 = offsets[e2] + r2
    tile_start = jnp.arange(n_tiles, dtype=jnp.int32) * tm
    te = jnp.sum((tile_start[:, None] >= ends[None, :]).astype(jnp.int32), axis=1)
    last_used = jnp.sum((ends[-1] > tile_start).astype(jnp.int32)) - 1
    te_last = jnp.sum((tile_start[last_used] >= ends).astype(jnp.int32))
    in_use = tile_start < ends[-1]
    te = jnp.where(in_use, jnp.minimum(te, N_EXPERTS - 1), te_last)
    tv = jnp.where(in_use, jnp.clip(offsets[te] + cnt[te] - tile_start, 0, tm), 0)
    return pos1, pos2, te.astype(jnp.int32), tv.astype(jnp.int32)


def _moe(x, g_all, wr_all, w1_all, w3_all, w2_all, layer, idx, tr, tm, tf, ts, tc):
    t, d = x.shape
    n_tiles = (2 * t) // tm + N_EXPERTS
    meta, counts = _router(x, g_all, wr_all, layer, idx, tr)
    pos1, pos2, te, tv = _routing_tables(meta, counts, tm, n_tiles)
    xs = _scatter_rows(x, pos1, pos2, n_tiles * tm, ts)
    ys = _experts(xs, te, tv, g_all, w1_all, w3_all, w2_all, layer, idx, tm, tf)
    return _combine(x, meta, ys, pos1, pos2, tc)


def _final_norm_kernel(x_ref, g_ref, o_ref):
    x = x_ref[...]
    o_ref[...] = x * _rms_scale(x) * g_ref[...]


def _final_norm(x, g, tm):
    t, d = x.shape
    return pl.pallas_call(
        _final_norm_kernel,
        grid=(t // tm,),
        in_specs=[pl.BlockSpec((tm, d), lambda i: (i, 0)),
                  pl.BlockSpec((1, d), lambda i: (0, 0))],
        out_specs=pl.BlockSpec((tm, d), lambda i: (i, 0)),
        out_shape=jax.ShapeDtypeStruct((t, d), F32),
        compiler_params=pltpu.CompilerParams(dimension_semantics=("arbitrary",)),
        name="final_norm",
    )(x, g.reshape(1, d))


def _pack_w_in(w_in):
    depth, d, _ = w_in.shape
    sizes = (ML_WIDTH, ML_WIDTH, ML_WIDTH, ML_WIDTH, ML_HEADS, ML_HEADS,
             MB_WIDTH, MB_CONV_DIM, MB_HEADS, RG_WIDTH, RG_WIDTH)
    bounds = np.concatenate([[0], np.cumsum(sizes)])
    seg = [w_in[:, :, int(bounds[k]):int(bounds[k + 1])] for k in range(len(sizes))]
    q, k, v, o, ig, fg, z, xbc, dt, rgg, rgx = seg
    gate = jnp.concatenate(
        [ig, fg, dt, dt, jnp.zeros((depth, d, LANES - (2 * ML_HEADS + 2 * MB_HEADS)), w_in.dtype)], axis=-1)
    pad = jnp.zeros((depth, d, N_PROJ - N_USED), w_in.dtype)
    return jnp.concatenate([q, k, v, o, z, xbc, rgg, rgx, gate, pad], axis=-1).astype(BF16)


def _block_diag(w):
    depth, g, n, _ = w.shape
    eye = jnp.eye(g, dtype=w.dtype)
    return jnp.einsum("lgij,gh->lgihj", w, eye).reshape(depth, g * n, g * n)


def _pad_lanes(v, offsets):
    depth = v[0].shape[0]
    row = jnp.zeros((depth, LANES), F32)
    for arr, off in zip(v, offsets):
        row = row.at[:, off:off + arr.shape[1]].set(arr.astype(F32))
    return row[:, None, :]


def _head_expansion():
    e = np.zeros((2, LANES, MB_WIDTH), np.float32)
    for h in range(MB_HEADS):
        e[0, G_DTA + h, h * MB_HEADDIM:(h + 1) * MB_HEADDIM] = 1.0
        e[1, G_DT + h, h * MB_HEADDIM:(h + 1) * MB_HEADDIM] = 1.0
    return jnp.asarray(e)


def _mixer_params(ml_conv_w, ml_conv_b, ml_b_i, ml_b_f, ml_norm_g, mb_conv_w, mb_conv_b,
                  mb_dt_bias, mb_a_log, mb_d, mb_norm_g, rg_conv_w, rg_conv_b,
                  rg_w_r, rg_b_r, rg_w_i, rg_b_i, rg_l):
    depth = ml_conv_w.shape[0]
    row = lambda a: a.astype(F32)[:, None, :]
    return dict(
        conv_w=jnp.concatenate([ml_conv_w, mb_conv_w, rg_conv_w], axis=-1).astype(F32),
        conv_b=row(jnp.concatenate([ml_conv_b, mb_conv_b, rg_conv_b], axis=-1)),
        gate_bias=_pad_lanes([ml_b_i, ml_b_f, mb_dt_bias, mb_dt_bias], [G_I, G_F, G_DTA, G_DT]),
        a_log=_pad_lanes([mb_a_log], [G_DTA]),
        ml_norm_g=row(ml_norm_g),
        mb_d=row(jnp.repeat(mb_d, MB_HEADDIM, axis=-1)),
        mb_norm_g=row(mb_norm_g),
        rg_wri=jnp.concatenate([_block_diag(rg_w_r), _block_diag(rg_w_i)], axis=-1).astype(BF16),
        rg_bri=row(jnp.concatenate([rg_b_r, rg_b_i], axis=-1)),
        rg_l=row(rg_l),
        ehead=_head_expansion(),
    )


def _forward(x, norm_mix_g, w_in, mixer_params, w_out, norm_ffn_g, ffn_w1, ffn_w3, ffn_w2,
             moe_router, moe_w1, moe_w3, moe_w2, norm_final_g, cfg):
    batch, seq, d = x.shape
    depth = w_in.shape[0]
    t = batch * seq
    xf = x.reshape(t, d)
    g_mix = norm_mix_g.astype(F32)[:, None, :]
    g_ffn = norm_ffn_g.astype(F32)[:, None, :]
    w_in_p = _pack_w_in(w_in)
    w_out_b = w_out.astype(BF16)
    ffn_w1b, ffn_w3b, ffn_w2b = ffn_w1.astype(BF16), ffn_w3.astype(BF16), ffn_w2.astype(BF16)
    moe_w1b, moe_w3b, moe_w2b = moe_w1.astype(BF16), moe_w3.astype(BF16), moe_w2.astype(BF16)
    n_e = moe_router.shape[-1]
    router_p = jnp.concatenate(
        [moe_router.astype(F32), jnp.zeros(moe_router.shape[:-1] + (LANES - n_e,), F32)], axis=-1)
    for l in range(depth):
        proj = _norm_matmul(xf, g_mix, w_in_p, l, cfg["tm_in"], cfg["tn_in"])
        y = _mixer(proj, mixer_params, l, batch, seq)
        xf = _matmul_residual(y, w_out_b, xf, l, cfg["tm_out"], cfg["tn_out"])
        if l % 2 == 0:
            xf = _ffn(xf, g_ffn, ffn_w1b, ffn_w3b, ffn_w2b, l, l // 2, cfg["tm_ffn"], cfg["tf_ffn"])
        else:
            xf = _moe(xf, g_ffn, router_p, moe_w1b, moe_w3b, moe_w2b, l, l // 2,
                      cfg["tr"], cfg["tm_moe"], cfg["tf_moe"], cfg["ts"], cfg["tc"])
    out = _final_norm(xf, norm_final_g.astype(F32), cfg["tm_norm"])
    return out.reshape(batch, seq, d)


_CFG = dict(tm_in=512, tn_in=512, tm_out=512, tn_out=1024, tm_ffn=512, tf_ffn=512,
            tr=512, tm_moe=512, tf_moe=512, ts=256, tc=256, tm_norm=512)


def kernel(x, norm_mix_g, w_in, ml_conv_w, ml_conv_b, ml_b_i, ml_b_f, ml_norm_g, mb_conv_w, mb_conv_b, mb_dt_bias, mb_a_log, mb_d, mb_norm_g, rg_conv_w, rg_conv_b, rg_w_r, rg_b_r, rg_w_i, rg_b_i, rg_l, w_out, norm_ffn_g, ffn_w1, ffn_w3, ffn_w2, moe_router, moe_w1, moe_w3, moe_w2, norm_final_g):
    mixer_params = _mixer_params(ml_conv_w, ml_conv_b, ml_b_i, ml_b_f, ml_norm_g, mb_conv_w,
                                 mb_conv_b, mb_dt_bias, mb_a_log, mb_d, mb_norm_g, rg_conv_w,
                                 rg_conv_b, rg_w_r, rg_b_r, rg_w_i, rg_b_i, rg_l)
    return _forward(x, norm_mix_g, w_in, mixer_params, w_out, norm_ffn_g, ffn_w1, ffn_w3, ffn_w2,
                    moe_router, moe_w1, moe_w3, moe_w2, norm_final_g, _CFG)
```
